```python
import jax, jax.numpy as jnp
from jax import lax
import numpy as np

D_MODEL = 1024
BATCH = 8
SEQ = 4096
DEPTH = 2

CTX_LEN = 256
GRID_W = 64
N_MIXERS = 2
N_HEADS = 8
N_KV_HEADS = 2
HEAD_DIM = D_MODEL // N_HEADS
Q_PER_KV = N_HEADS // N_KV_HEADS
KV_WIDTH = N_KV_HEADS * HEAD_DIM
ROPE_THETA = 10000.0
Q_BLOCK = 128
N_FOURIER_GROUPS = 4
FOURIER_GROUP = D_MODEL // N_FOURIER_GROUPS
D_FF = -(-8 * D_MODEL // (3 * 256)) * 256
N_MOD = 6
N_ATTN_LAYERS = (DEPTH + 1) // 2
N_FOURIER_LAYERS = DEPTH // 2
EPS = 1e-6

kernel_name = "hybrid_attn_fourier_dit_block"


def rms_norm(x, g):
    x32 = x.astype(jnp.float32)
    y = x32 * lax.rsqrt(jnp.mean(x32 * x32, axis=-1, keepdims=True) + EPS)
    return y.astype(x.dtype) * g


def modulate(h, shift, scale):
    return h * (1 + scale) + shift


def axial_rope_angles(rows):
    row = jnp.repeat(jnp.arange(rows), GRID_W).astype(jnp.float32)
    col = jnp.tile(jnp.arange(GRID_W), rows).astype(jnp.float32)
    n_freq = HEAD_DIM // 4
    inv_freq = ROPE_THETA ** (-jnp.arange(n_freq, dtype=jnp.float32) / n_freq)
    ang = jnp.concatenate([row[:, None] * inv_freq, col[:, None] * inv_freq], axis=-1)
    return jnp.cos(ang), jnp.sin(ang)


def apply_rope(x, cos, sin):
    half = HEAD_DIM // 2
    cos = cos[None, :, None, :].astype(x.dtype)
    sin = sin[None, :, None, :].astype(x.dtype)
    x1, x2 = x[..., :half], x[..., half:]
    return jnp.concatenate([x1 * cos - x2 * sin, x1 * sin + x2 * cos], axis=-1)


def gqa_attend(q, k, v):
    s = jnp.einsum('bqkgd,bskd->bkgqs', q, k, preferred_element_type=jnp.float32) * (HEAD_DIM ** -0.5)
    p = jax.nn.softmax(s, axis=-1).astype(v.dtype)
    return jnp.einsum('bkgqs,bskd->bqkgd', p, v)


def qkv_project(h, w_qkv, g_q, g_k):
    b, n, _ = h.shape
    qkv = h @ w_qkv
    q, k, v = jnp.split(qkv, [D_MODEL, D_MODEL + KV_WIDTH], axis=-1)
    q = rms_norm(q.reshape(b, n, N_HEADS, HEAD_DIM), g_q)
    k = rms_norm(k.reshape(b, n, N_KV_HEADS, HEAD_DIM), g_k)
    v = v.reshape(b, n, N_KV_HEADS, HEAD_DIM)
    return q, k, v


def attention_mixer(hx, hc, w_qkv, g_q, g_k, w_o, cos, sin, with_ctx_queries):
    b, s, _ = hx.shape
    n_ctx = hc.shape[1]
    qx, kx, vx = qkv_project(hx, w_qkv, g_q, g_k)
    qx = apply_rope(qx, cos, sin)
    kx = apply_rope(kx, cos, sin)
    qc, kc, vc = qkv_project(hc, w_qkv, g_q, g_k)
    k_all = jnp.concatenate([kx, kc], axis=1)
    v_all = jnp.concatenate([vx, vc], axis=1)
    nb = s // Q_BLOCK
    qb = qx.reshape(b, nb, Q_BLOCK, N_KV_HEADS, Q_PER_KV, HEAD_DIM).transpose(1, 0, 2, 3, 4, 5)
    ox = lax.map(lambda q_blk: gqa_attend(q_blk, k_all, v_all), qb)
    ox = ox.transpose(1, 0, 2, 3, 4, 5).reshape(b, s, D_MODEL) @ w_o
    oc = None
    if with_ctx_queries:
        qc = qc.reshape(b, n_ctx, N_KV_HEADS, Q_PER_KV, HEAD_DIM)
        oc = gqa_attend(qc, kc, vc).reshape(b, n_ctx, D_MODEL) @ w_o
    return ox, oc


def fourier_mixer(h, w_f, b_f):
    b, n, _ = h.shape
    hg = h.astype(jnp.float32).reshape(b, n, N_FOURIER_GROUPS, FOURIER_GROUP)
    f = jnp.fft.fft2(hg, axes=(1, 3), norm='ortho').real
    return f.reshape(b, n, D_MODEL).astype(h.dtype) @ w_f + b_f


def swiglu(h, w_gate_up, w_down):
    g, u = jnp.split(h @ w_gate_up, 2, axis=-1)
    return (jax.nn.silu(g) * u) @ w_down


def setup_inputs(seed: int = 0) -> dict:
    key = jax.random.key(seed)
    ks = jax.random.split(key, 20)
    f32 = jnp.float32
    nrm = lambda k, shape, s: jax.random.normal(k, shape, f32) * s
    D = D_MODEL
    return {
        'x': nrm(ks[0], (BATCH, SEQ, D), 1.0),
        'c': nrm(ks[1], (BATCH, D), 1.0),
        'ctx': nrm(ks[2], (BATCH, CTX_LEN, D), 1.0),
        'c_ctx': nrm(ks[3], (D,), 1.0),
        'w_mod': nrm(ks[4], (DEPTH, D, N_MOD * D), 0.5 * D ** -0.5),
        'b_mod': nrm(ks[5], (DEPTH, N_MOD * D), 0.01),
        'g_mix': 1.0 + nrm(ks[6], (DEPTH, D), 0.05),
        'g_ffn': 1.0 + nrm(ks[7], (DEPTH, D), 0.05),
        'w_qkv': nrm(ks[8], (N_ATTN_LAYERS, D, D + 2 * KV_WIDTH), D ** -0.5),
        'g_q': 1.0 + nrm(ks[9], (N_ATTN_LAYERS, HEAD_DIM), 0.05),
        'g_k': 1.0 + nrm(ks[10], (N_ATTN_LAYERS, HEAD_DIM), 0.05),
        'w_attn_out': nrm(ks[11], (N_ATTN_LAYERS, D, D), D ** -0.5),
        'w_fourier': nrm(ks[12], (N_FOURIER_LAYERS, D, D), D ** -0.5),
        'b_fourier': nrm(ks[13], (N_FOURIER_LAYERS, D), 0.01),
        'w_gate_up': nrm(ks[14], (DEPTH, D, 2 * D_FF), D ** -0.5),
        'w_down': nrm(ks[15], (DEPTH, D_FF, D), D_FF ** -0.5),
        'g_final': 1.0 + nrm(ks[16], (D,), 0.05),
    }


def reference(x, c, ctx, c_ctx, w_mod, b_mod, g_mix, g_ffn, w_qkv, g_q, g_k, w_attn_out,
              w_fourier, b_fourier, w_gate_up, w_down, g_final):
    b, s, d = x.shape
    ROWS = s // GRID_W
    cos, sin = axial_rope_angles(ROWS)
    silu_c = jax.nn.silu(c)
    silu_cc = jax.nn.silu(c_ctx)
    h_ctx = ctx
    for i in range(DEPTH):
        update_ctx = i < DEPTH - 1
        is_attn = i % N_MIXERS == 0
        j = i // N_MIXERS
        mx = (silu_c @ w_mod[i] + b_mod[i]).reshape(b, N_MOD, 1, d)
        mc = (silu_cc @ w_mod[i] + b_mod[i]).reshape(N_MOD, d)
        hx = modulate(rms_norm(x, g_mix[i]), mx[:, 0], mx[:, 1])
        hc = modulate(rms_norm(h_ctx, g_mix[i]), mc[0], mc[1]) if (update_ctx or is_attn) else None
        if is_attn:
            ox, oc = attention_mixer(hx, hc, w_qkv[j], g_q[j], g_k[j], w_attn_out[j], cos, sin, update_ctx)
        else:
            ox = fourier_mixer(hx, w_fourier[j], b_fourier[j])
            oc = fourier_mixer(hc, w_fourier[j], b_fourier[j]) if update_ctx else None
        x = x + mx[:, 2] * ox
        x = x + mx[:, 5] * swiglu(modulate(rms_norm(x, g_ffn[i]), mx[:, 3], mx[:, 4]), w_gate_up[i], w_down[i])
        if update_ctx:
            h_ctx = h_ctx + mc[2] * oc
            h_ctx = h_ctx + mc[5] * swiglu(modulate(rms_norm(h_ctx, g_ffn[i]), mc[3], mc[4]), w_gate_up[i], w_down[i])
    return rms_norm(x, g_final)
```

```python
import functools
import math

import numpy as np
import jax
import jax.numpy as jnp
from jax import lax
from jax.experimental import pallas as pl
from jax.experimental.pallas import tpu as pltpu

F32 = jnp.float32
BF16 = jnp.bfloat16

EPS = 1e-6
HEAD_DIM = 128
N_HEADS = 8
N_KV_HEADS = 2
Q_PER_KV = N_HEADS // N_KV_HEADS
KV_WIDTH = N_KV_HEADS * HEAD_DIM
ROPE_THETA = 10000.0
GRID_W = 64
N_MOD = 6
FOURIER_GROUP = 256
FF_CHUNK = 256
KEY_CHUNK = 512
DFT_RADIX = 64
DFT1_ROWS = 16
DFT2_SLABS = 8
MOD_ROWS = 16
LOG2E = math.log2(math.e)
NEG_BIG = -1e30
VMEM_LIMIT = 56 * 1024 * 1024


def _cparams(n_axes):
    return pltpu.CompilerParams(
        dimension_semantics=("parallel",) * n_axes, vmem_limit_bytes=VMEM_LIMIT)


def _rms_mod(x, gain, shift, scale):
    inv = lax.rsqrt(jnp.mean(x * x, axis=-1, keepdims=True) + EPS)
    return (x * inv * gain) * (1.0 + scale) + shift


def _silu(x):
    return x * (1.0 / (1.0 + jnp.exp(-x)))


def _mod_kernel(cc_ref, w_ref, b_ref, o_ref):
    s = _silu(cc_ref[...])
    o_ref[0] = jnp.dot(s, w_ref[0], preferred_element_type=F32,
                       precision=lax.Precision.HIGHEST) + b_ref[0]


def _modulation(cc, w_mod, b_mod):
    depth, d, n = w_mod.shape
    tn = 1536
    return pl.pallas_call(
        _mod_kernel,
        grid=(depth, n // tn),
        in_specs=[
            pl.BlockSpec((MOD_ROWS, d), lambda i, j: (0, 0)),
            pl.BlockSpec((1, d, tn), lambda i, j: (i, 0, j)),
            pl.BlockSpec((1, 1, tn), lambda i, j: (i, 0, j)),
        ],
        out_specs=pl.BlockSpec((1, MOD_ROWS, tn), lambda i, j: (i, 0, j)),
        out_shape=jax.ShapeDtypeStruct((depth, MOD_ROWS, n), F32),
        compiler_params=_cparams(2),
        name="modulation",
    )(cc, w_mod, b_mod.reshape(depth, 1, n))


def _qkv_kernel(x_ref, mod_ref, gmix_ref, w_ref, gq_ref, gk_ref, cos_ref, sin_ref,
                q_ref, k_ref, vt_ref, *, use_rope):
    d = x_ref.shape[2]
    h = _rms_mod(x_ref[0], gmix_ref[...], mod_ref[0, 0:1, :], mod_ref[0, 1:2, :])
    qkv = jnp.dot(h.astype(BF16), w_ref[...], preferred_element_type=F32)

    def norm_head(z, gain, scale):
        inv = lax.rsqrt(jnp.mean(z * z, axis=-1, keepdims=True) + EPS)
        zn = (z * inv) * gain
        if use_rope:
            zn = zn * cos_ref[...] + pltpu.roll(zn, HEAD_DIM // 2, axis=1) * sin_ref[...]
        return zn * scale

    q_scale = HEAD_DIM ** -0.5 * LOG2E
    for hh in range(N_HEADS):
        sl = slice(hh * HEAD_DIM, (hh + 1) * HEAD_DIM)
        q_ref[0, :, sl] = norm_head(qkv[:, sl], gq_ref[...], q_scale).astype(BF16)
    for j in range(N_KV_HEADS):
        sl = slice(j * HEAD_DIM, (j + 1) * HEAD_DIM)
        ksl = slice(d + j * HEAD_DIM, d + (j + 1) * HEAD_DIM)
        vsl = slice(d + KV_WIDTH + j * HEAD_DIM, d + KV_WIDTH + (j + 1) * HEAD_DIM)
        k_ref[0, :, sl] = norm_head(qkv[:, ksl], gk_ref[...], 1.0).astype(BF16)
        vt_ref[0, j, 0] = qkv[:, vsl].T.astype(BF16)


def _qkv_project(x, mod, mod_row, gmix, w_qkv, g_q, g_k, cos2, sin2, *, tm, use_rope):
    b, n, d = x.shape
    nt = n // tm
    return pl.pallas_call(
        functools.partial(_qkv_kernel, use_rope=use_rope),
        grid=(b, nt),
        in_specs=[
            pl.BlockSpec((1, tm, d), lambda bi, i: (bi, i, 0)),
            pl.BlockSpec((1, N_MOD, d), lambda bi, i: (mod_row(bi), 0, 0)),
            pl.BlockSpec((1, d), lambda bi, i: (0, 0)),
            pl.BlockSpec(w_qkv.shape, lambda bi, i: (0, 0)),
            pl.BlockSpec((1, HEAD_DIM), lambda bi, i: (0, 0)),
            pl.BlockSpec((1, HEAD_DIM), lambda bi, i: (0, 0)),
            pl.BlockSpec((tm, HEAD_DIM), lambda bi, i: (i, 0)),
            pl.BlockSpec((tm, HEAD_DIM), lambda bi, i: (i, 0)),
        ],
        out_specs=[
            pl.BlockSpec((1, tm, d), lambda bi, i: (bi, i, 0)),
            pl.BlockSpec((1, tm, KV_WIDTH), lambda bi, i: (bi, i, 0)),
            pl.BlockSpec((1, N_KV_HEADS, 1, HEAD_DIM, tm), lambda bi, i: (bi, 0, i, 0, 0)),
        ],
        out_shape=[
            jax.ShapeDtypeStruct((b, n, d), BF16),
            jax.ShapeDtypeStruct((b, n, KV_WIDTH), BF16),
            jax.ShapeDtypeStruct((b, N_KV_HEADS, nt, HEAD_DIM, tm), BF16),
        ],
        compiler_params=_cparams(2),
        name="qkv_rope" if use_rope else "qkv_ctx",
    )(x, mod, gmix, w_qkv, g_q, g_k, cos2, sin2)


def _attn_init(m_ref, l_ref, acc_ref):
    m_ref[...] = jnp.full(m_ref.shape, NEG_BIG, F32)
    l_ref[...] = jnp.zeros(l_ref.shape, F32)
    acc_ref[...] = jnp.zeros(acc_ref.shape, F32)


def _attn_update(hh, qh, kblk, vtblk, m_ref, l_ref, acc_ref):
    s = lax.dot_general(kblk, qh, (((1,), (1,)), ((), ())), preferred_element_type=F32)
    m_old = m_ref[hh]
    m_new = jnp.maximum(m_old, jnp.max(s, axis=0, keepdims=True))
    alpha = jnp.exp2(m_old - m_new)
    p = jnp.exp2(s - m_new)
    l_ref[hh] = alpha * l_ref[hh] + jnp.sum(p, axis=0, keepdims=True)
    acc_ref[hh] = alpha * acc_ref[hh] + jnp.dot(vtblk, p.astype(BF16), preferred_element_type=F32)
    m_ref[hh] = m_new


def _attn_finish(o_ref, l_ref, acc_ref):
    for hh in range(Q_PER_KV):
        o_t = acc_ref[hh] * (1.0 / l_ref[hh])
        o_ref[0, :, hh * HEAD_DIM:(hh + 1) * HEAD_DIM] = o_t.T.astype(BF16)


def _attn_kernel(q_ref, k_ref, vt_ref, kc_ref, vtc_ref, o_ref, m_ref, l_ref, acc_ref):
    _attn_init(m_ref, l_ref, acc_ref)
    n_chunks = vt_ref.shape[2]

    def body(c, carry):
        start = pl.multiple_of(c * KEY_CHUNK, KEY_CHUNK)
        kblk = k_ref[0, pl.ds(start, KEY_CHUNK), :]
        vtblk = vt_ref[0, 0, c]
        for hh in range(Q_PER_KV):
            qh = q_ref[0, :, hh * HEAD_DIM:(hh + 1) * HEAD_DIM]
            _attn_update(hh, qh, kblk, vtblk, m_ref, l_ref, acc_ref)
        return carry

    lax.fori_loop(0, n_chunks, body, 0)
    for hh in range(Q_PER_KV):
        qh = q_ref[0, :, hh * HEAD_DIM:(hh + 1) * HEAD_DIM]
        _attn_update(hh, qh, kc_ref[0], vtc_ref[0, 0, 0], m_ref, l_ref, acc_ref)
    _attn_finish(o_ref, l_ref, acc_ref)


def _ctx_attn_kernel(q_ref, kc_ref, vtc_ref, o_ref, m_ref, l_ref, acc_ref):
    _attn_init(m_ref, l_ref, acc_ref)
    for hh in range(Q_PER_KV):
        qh = q_ref[0, :, hh * HEAD_DIM:(hh + 1) * HEAD_DIM]
        _attn_update(hh, qh, kc_ref[0], vtc_ref[0, 0, 0], m_ref, l_ref, acc_ref)
    _attn_finish(o_ref, l_ref, acc_ref)


def _attn_scratch(tq):
    return [pltpu.VMEM((Q_PER_KV, 1, tq), F32), pltpu.VMEM((Q_PER_KV, 1, tq), F32),
            pltpu.VMEM((Q_PER_KV, HEAD_DIM, tq), F32)]


def _attention(q, k, vt, kc, vtc, *, tq):
    b, s, d = q.shape
    n_ctx = kc.shape[1]
    gw = Q_PER_KV * HEAD_DIM
    n_chunks = vt.shape[2]
    assert vt.shape[4] == KEY_CHUNK and n_chunks * KEY_CHUNK == s
    return pl.pallas_call(
        _attn_kernel,
        grid=(b, N_KV_HEADS, s // tq),
        in_specs=[
            pl.BlockSpec((1, tq, gw), lambda bi, g, i: (bi, i, g)),
            pl.BlockSpec((1, s, HEAD_DIM), lambda bi, g, i: (bi, 0, g)),
            pl.BlockSpec((1, 1, n_chunks, HEAD_DIM, KEY_CHUNK), lambda bi, g, i: (bi, g, 0, 0, 0)),
            pl.BlockSpec((1, n_ctx, HEAD_DIM), lambda bi, g, i: (bi, 0, g)),
            pl.BlockSpec((1, 1, 1, HEAD_DIM, n_ctx), lambda bi, g, i: (bi, g, 0, 0, 0)),
        ],
        out_specs=pl.BlockSpec((1, tq, gw), lambda bi, g, i: (bi, i, g)),
        out_shape=jax.ShapeDtypeStruct((b, s, d), BF16),
        scratch_shapes=_attn_scratch(tq),
        compiler_params=_cparams(3),
        name="attention",
    )(q, k, vt, kc, vtc)


def _ctx_attention(qc, kc, vtc):
    b, n_ctx, d = qc.shape
    gw = Q_PER_KV * HEAD_DIM
    return pl.pallas_call(
        _ctx_attn_kernel,
        grid=(b, N_KV_HEADS),
        in_specs=[
            pl.BlockSpec((1, n_ctx, gw), lambda bi, g: (bi, 0, g)),
            pl.BlockSpec((1, n_ctx, HEAD_DIM), lambda bi, g: (bi, 0, g)),
            pl.BlockSpec((1, 1, 1, HEAD_DIM, n_ctx), lambda bi, g: (bi, g, 0, 0, 0)),
        ],
        out_specs=pl.BlockSpec((1, n_ctx, gw), lambda bi, g: (bi, 0, g)),
        out_shape=jax.ShapeDtypeStruct((b, n_ctx, d), BF16),
        scratch_shapes=_attn_scratch(n_ctx),
        compiler_params=_cparams(2),
        name="ctx_attention",
    )(qc, kc, vtc)


def _post_kernel(x_ref, o_ref, mod_ref, wo_ref, bo_ref, gffn_ref, wgu_ref, wd_ref, gfin_ref,
                 out_ref, hn_ref, acc_ref, *, final_norm):
    proj = jnp.dot(o_ref[0], wo_ref[...], preferred_element_type=F32) + bo_ref[...]
    x1 = x_ref[0] + mod_ref[0, 2:3, :] * proj
    out_ref[0] = x1
    hn = _rms_mod(x1, gffn_ref[...], mod_ref[0, 3:4, :], mod_ref[0, 4:5, :])
    hn_ref[...] = hn.astype(BF16)
    acc_ref[...] = jnp.zeros(acc_ref.shape, F32)

    def body(c, carry):
        gu = jnp.dot(hn_ref[...], wgu_ref[c], preferred_element_type=F32)
        act = _silu(gu[:, :FF_CHUNK]) * gu[:, FF_CHUNK:]
        acc_ref[...] += jnp.dot(act.astype(BF16), wd_ref[c], preferred_element_type=F32)
        return carry

    lax.fori_loop(0, wgu_ref.shape[0], body, 0)
    y = out_ref[0] + mod_ref[0, 5:6, :] * acc_ref[...]
    if final_norm:
        y = y * lax.rsqrt(jnp.mean(y * y, axis=-1, keepdims=True) + EPS) * gfin_ref[...]
    out_ref[0] = y


def _post_mixer(x, o, mod, mod_row, w_o, b_o, g_ffn, wgu, wd, g_fin, *, tm, final_norm):
    b, n, d = x.shape
    const2 = lambda bi, i: (0, 0)
    const3 = lambda bi, i: (0, 0, 0)
    single = pl.Buffered(1)
    return pl.pallas_call(
        functools.partial(_post_kernel, final_norm=final_norm),
        grid=(b, n // tm),
        in_specs=[
            pl.BlockSpec((1, tm, d), lambda bi, i: (bi, i, 0)),
            pl.BlockSpec((1, tm, d), lambda bi, i: (bi, i, 0)),
            pl.BlockSpec((1, N_MOD, d), lambda bi, i: (mod_row(bi), 0, 0)),
            pl.BlockSpec(w_o.shape, const2, pipeline_mode=single),
            pl.BlockSpec((1, d), const2),
            pl.BlockSpec((1, d), const2),
            pl.BlockSpec(wgu.shape, const3, pipeline_mode=single),
            pl.BlockSpec(wd.shape, const3, pipeline_mode=single),
            pl.BlockSpec((1, d), const2),
        ],
        out_specs=pl.BlockSpec((1, tm, d), lambda bi, i: (bi, i, 0)),
        out_shape=jax.ShapeDtypeStruct((b, n, d), F32),
        scratch_shapes=[pltpu.VMEM((tm, d), BF16), pltpu.VMEM((tm, d), F32)],
        compiler_params=_cparams(2),
        name="post_final" if final_norm else "post_mixer",
    )(x, o, mod, w_o, b_o, g_ffn, wgu, wd, g_fin)


def _dft1_kernel(x_ref, mod_ref, gmix_ref, g1_ref, twr_ref, twi_ref, t_ref):
    r, g, d = x_ref.shape[1:]
    rows = r * g
    x = x_ref[0].reshape(rows, d)
    h = _rms_mod(x, gmix_ref[...], mod_ref[0, 0:1, :], mod_ref[0, 1:2, :]).astype(BF16)
    a_re = jnp.dot(g1_ref[0], h, preferred_element_type=F32)
    a_im = jnp.dot(g1_ref[1], h, preferred_element_type=F32)
    c_re = twr_ref[...].reshape(rows, 128)
    c_im = twi_ref[...].reshape(rows, 128)
    for j in range(d // 128):
        sl = slice(j * 128, (j + 1) * 128)
        t_re = a_re[:, sl] * c_re - a_im[:, sl] * c_im
        t_im = a_re[:, sl] * c_im + a_im[:, sl] * c_re
        t_ref[0, 0, :, :, sl] = t_re.reshape(r, g, 128).astype(BF16)
        t_ref[0, 1, :, :, sl] = t_im.reshape(r, g, 128).astype(BF16)


def _dft2_kernel(t_ref, f2_ref, wc_ref, o_ref, y_ref):
    slabs, r, d = o_ref.shape[1:]
    n_groups = d // FOURIER_GROUP
    for j in range(slabs):
        u = jnp.concatenate([t_ref[0, 0, j], t_ref[0, 1, j]], axis=0)
        y = jnp.dot(f2_ref[...], u, preferred_element_type=F32).astype(BF16)
        for gi in range(n_groups):
            sl = slice(gi * FOURIER_GROUP, (gi + 1) * FOURIER_GROUP)
            y_ref[gi, j * r:(j + 1) * r, :FOURIER_GROUP] = y[:r, sl]
            y_ref[gi, j * r:(j + 1) * r, FOURIER_GROUP:] = y[r:, sl]
    for gi in range(n_groups):
        f = jnp.dot(y_ref[gi], wc_ref[...], preferred_element_type=F32)
        o_ref[0, :, :, gi * FOURIER_GROUP:(gi + 1) * FOURIER_GROUP] = (
            f.reshape(slabs, r, FOURIER_GROUP).astype(BF16))


def _dft_constants(s):
    r = DFT_RADIX
    assert r * r == s
    idx = np.arange(r)
    ang = 2.0 * np.pi * np.outer(idx, idx) / r
    eye = np.eye(DFT1_ROWS)
    norm = 1.0 / math.sqrt(s)
    g1 = np.stack([np.kron(np.cos(ang), eye), np.kron(-np.sin(ang), eye)]) * norm
    tw = 2.0 * np.pi * np.outer(idx, idx) / s
    twr = np.broadcast_to(np.cos(tw)[:, :, None], (r, r, 128))
    twi = np.broadcast_to(-np.sin(tw)[:, :, None], (r, r, 128))
    c2, s2 = np.cos(ang), np.sin(ang)
    f2 = np.block([[c2, s2], [-s2, c2]])
    gidx = np.arange(FOURIER_GROUP)
    gang = 2.0 * np.pi * np.outer(gidx, gidx) / FOURIER_GROUP
    wc = np.concatenate([np.cos(gang), np.sin(gang)], axis=0) / math.sqrt(FOURIER_GROUP)
    as_bf16 = lambda a: jnp.asarray(a, F32).astype(BF16)
    return (as_bf16(g1), jnp.asarray(twr, F32), jnp.asarray(twi, F32), as_bf16(f2), as_bf16(wc))


def _fourier_mix(x, mod, gmix):
    b, s, d = x.shape
    r = DFT_RADIX
    g1, twr, twi, f2, wc = _dft_constants(s)
    x4 = x.reshape(b, r, r, d)
    t = pl.pallas_call(
        _dft1_kernel,
        grid=(b, r // DFT1_ROWS),
        in_specs=[
            pl.BlockSpec((1, r, DFT1_ROWS, d), lambda bi, i: (bi, 0, i, 0)),
            pl.BlockSpec((1, N_MOD, d), lambda bi, i: (bi, 0, 0)),
            pl.BlockSpec((1, d), lambda bi, i: (0, 0)),
            pl.BlockSpec(g1.shape, lambda bi, i: (0, 0, 0)),
            pl.BlockSpec((r, DFT1_ROWS, 128), lambda bi, i: (0, i, 0)),
            pl.BlockSpec((r, DFT1_ROWS, 128), lambda bi, i: (0, i, 0)),
        ],
        out_specs=pl.BlockSpec((1, 2, r, DFT1_ROWS, d), lambda bi, i: (bi, 0, 0, i, 0)),
        out_shape=jax.ShapeDtypeStruct((b, 2, r, r, d), BF16),
        compiler_params=_cparams(2),
        name="dft_stage1",
    )(x4, mod, gmix, g1, twr, twi)
    fo = pl.pallas_call(
        _dft2_kernel,
        grid=(b, r // DFT2_SLABS),
        in_specs=[
            pl.BlockSpec((1, 2, DFT2_SLABS, r, d), lambda bi, i: (bi, 0, i, 0, 0)),
            pl.BlockSpec(f2.shape, lambda bi, i: (0, 0)),
            pl.BlockSpec(wc.shape, lambda bi, i: (0, 0)),
        ],
        out_specs=pl.BlockSpec((1, DFT2_SLABS, r, d), lambda bi, i: (bi, i, 0, 0)),
        out_shape=jax.ShapeDtypeStruct((b, r, r, d), BF16),
        scratch_shapes=[pltpu.VMEM((d // FOURIER_GROUP, DFT2_SLABS * r, 2 * FOURIER_GROUP), BF16)],
        compiler_params=_cparams(2),
        name="dft_stage2",
    )(t, f2, wc)
    return fo.transpose(0, 2, 1, 3).reshape(b, s, d)


def _rope_tables(s):
    rows = s // GRID_W
    row = jnp.repeat(jnp.arange(rows), GRID_W).astype(F32)
    col = jnp.tile(jnp.arange(GRID_W), rows).astype(F32)
    n_freq = HEAD_DIM // 4
    inv_freq = ROPE_THETA ** (-jnp.arange(n_freq, dtype=F32) / n_freq)
    ang = jnp.concatenate([row[:, None] * inv_freq, col[:, None] * inv_freq], axis=-1)
    cos, sin = jnp.cos(ang), jnp.sin(ang)
    return jnp.concatenate([cos, cos], axis=-1), jnp.concatenate([-sin, sin], axis=-1)


def _ffn_weights(w_gate_up, w_down):
    d, two_ff = w_gate_up.shape
    d_ff = two_ff // 2
    nc = d_ff // FF_CHUNK
    gate = w_gate_up[:, :d_ff].reshape(d, nc, FF_CHUNK)
    up = w_gate_up[:, d_ff:].reshape(d, nc, FF_CHUNK)
    wgu = jnp.concatenate([gate, up], axis=-1).transpose(1, 0, 2).astype(BF16)
    wd = w_down.reshape(nc, FF_CHUNK, d).astype(BF16)
    return wgu, wd


def kernel(x, c, ctx, c_ctx, w_mod, b_mod, g_mix, g_ffn, w_qkv, g_q, g_k, w_attn_out,
           w_fourier, b_fourier, w_gate_up, w_down, g_final):
    b, s, d = x.shape
    n_ctx = ctx.shape[1]
    depth = w_mod.shape[0]
    assert depth == 2 and b + 1 <= MOD_ROWS

    cc = jnp.concatenate([c, c_ctx[None, :], jnp.zeros((MOD_ROWS - b - 1, d), F32)], axis=0)
    mod = _modulation(cc, w_mod, b_mod).reshape(depth, MOD_ROWS, N_MOD, d)
    lat_row = lambda bi: bi
    ctx_row = lambda bi: b
    row = lambda v: v.reshape(1, -1)
    g_fin = row(g_final)
    zero_bias = jnp.zeros((1, d), F32)

    cos2, sin2 = _rope_tables(s)
    wqkv = w_qkv[0].astype(BF16)
    q, k, vt = _qkv_project(x, mod[0], lat_row, row(g_mix[0]), wqkv, row(g_q[0]), row(g_k[0]),
                            cos2, sin2, tm=KEY_CHUNK, use_rope=True)
    qc, kc, vtc = _qkv_project(ctx, mod[0], ctx_row, row(g_mix[0]), wqkv, row(g_q[0]), row(g_k[0]),
                               cos2, sin2, tm=n_ctx, use_rope=False)
    o = _attention(q, k, vt, kc, vtc, tq=256)
    wgu0, wd0 = _ffn_weights(w_gate_up[0], w_down[0])
    wo = w_attn_out[0].astype(BF16)
    x = _post_mixer(x, o, mod[0], lat_row, wo, zero_bias, row(g_ffn[0]), wgu0, wd0, g_fin,
                    tm=512, final_norm=False)
    oc = _ctx_attention(qc, kc, vtc)
    h_ctx = _post_mixer(ctx, oc, mod[0], ctx_row, wo, zero_bias, row(g_ffn[0]), wgu0, wd0, g_fin,
                        tm=n_ctx, final_norm=False)
    del h_ctx

    f = _fourier_mix(x, mod[1], row(g_mix[1]))
    wgu1, wd1 = _ffn_weights(w_gate_up[1], w_down[1])
    return _post_mixer(x, f, mod[1], lat_row, w_fourier[0].astype(BF16), row(b_fourier[0]),
                       row(g_ffn[1]), wgu1, wd1, g_fin, tm=512, final_norm=True)
```

```python
import functools
import math

import numpy as np
import jax
import jax.numpy as jnp
from jax import lax
from jax.experimental import pallas as pl
from jax.experimental.pallas import tpu as pltpu

F32 = jnp.float32
BF16 = jnp.bfloat16

EPS = 1e-6
HEAD_DIM = 128
N_HEADS = 8
N_KV_HEADS = 2
Q_PER_KV = N_HEADS // N_KV_HEADS
KV_WIDTH = N_KV_HEADS * HEAD_DIM
ROPE_THETA = 10000.0
GRID_W = 64
N_MOD = 6
FOURIER_GROUP = 256
FF_CHUNK = 256
KEY_CHUNK = 4096
QUERY_LANES = 512
QKV_ROWS = 512
DFT_RADIX = 64
DFT1_ROWS = 16
DFT2_SLABS = 8
MOD_ROWS = 16
LOG2E = math.log2(math.e)
VMEM_LIMIT = 56 * 1024 * 1024


def _cparams(n_axes, flags=None):
    return pltpu.CompilerParams(
        dimension_semantics=("parallel",) * n_axes, vmem_limit_bytes=VMEM_LIMIT, flags=flags)


def _rms_mod(x, gain, shift, scale):
    inv = lax.rsqrt(jnp.mean(x * x, axis=-1, keepdims=True) + EPS)
    return (x * inv * gain) * (1.0 + scale) + shift


def _silu(x):
    return x * (1.0 / (1.0 + jnp.exp(-x)))


def _mod_kernel(cc_ref, w_ref, b_ref, o_ref):
    s = _silu(cc_ref[...])
    o_ref[0] = jnp.dot(s, w_ref[0], preferred_element_type=F32,
                       precision=lax.Precision.HIGHEST) + b_ref[0]


def _modulation(cc, w_mod, b_mod):
    depth, d, n = w_mod.shape
    tn = 1536
    return pl.pallas_call(
        _mod_kernel,
        grid=(depth, n // tn),
        in_specs=[
            pl.BlockSpec((MOD_ROWS, d), lambda i, j: (0, 0)),
            pl.BlockSpec((1, d, tn), lambda i, j: (i, 0, j)),
            pl.BlockSpec((1, 1, tn), lambda i, j: (i, 0, j)),
        ],
        out_specs=pl.BlockSpec((1, MOD_ROWS, tn), lambda i, j: (i, 0, j)),
        out_shape=jax.ShapeDtypeStruct((depth, MOD_ROWS, n), F32),
        compiler_params=_cparams(2),
        name="modulation",
    )(cc, w_mod, b_mod.reshape(depth, 1, n))


def _qkv_kernel(x_ref, mod_ref, gmix_ref, w_ref, gq_ref, gk_ref, cos_ref, sin_ref, ones_ref, perm_ref,
                q_ref, k_ref, vt_ref, *, use_rope):
    h = _rms_mod(x_ref[0], gmix_ref[...], mod_ref[0, 0:1, :], mod_ref[0, 1:2, :]).astype(BF16)
    pair = 2 * HEAD_DIM
    n_q, n_k = N_HEADS // 2, N_KV_HEADS // 2
    n_norm, n_tiles = n_q + n_k, w_ref.shape[1] // pair
    q_scale = HEAD_DIM ** -0.5 * LOG2E

    def project(t):
        return jnp.dot(h, w_ref[:, t * pair:(t + 1) * pair], preferred_element_type=F32)

    def head_sums(z):
        zz = z * z
        hi = zz.astype(BF16)
        lo = (zz - hi.astype(F32)).astype(BF16)
        return (jnp.dot(hi, ones_ref[...], preferred_element_type=F32)
                + jnp.dot(lo, ones_ref[...], preferred_element_type=F32))

    def normalise(t, z, ss):
        gain = gq_ref[...] if t < n_q else gk_ref[...]
        zn = (z * lax.rsqrt(ss * (1.0 / HEAD_DIM) + EPS)) * gain
        rolled = jnp.dot(zn.astype(BF16), perm_ref[...], preferred_element_type=F32) if use_rope else None
        return zn, rolled

    def store(t, zn, rolled):
        if use_rope:
            zn = zn * cos_ref[...] + rolled * sin_ref[...]
        if t < n_q:
            q_ref[0, :, t * pair:(t + 1) * pair] = (zn * q_scale).astype(BF16)
        else:
            k_ref[0, :, (t - n_q) * pair:(t - n_q + 1) * pair] = zn.astype(BF16)

    z, ss, normed = {}, {}, {}
    for step in range(n_tiles + 3):
        if step < n_tiles:
            z[step] = project(step)
        t = step - 1
        if 0 <= t < n_norm:
            ss[t] = head_sums(z[t])
        elif n_norm <= t < n_tiles:
            for half in range(2):
                vt_ref[0, 2 * (t - n_norm) + half] = (
                    z[t][:, half * HEAD_DIM:(half + 1) * HEAD_DIM].T.astype(BF16))
        t = step - 2
        if 0 <= t < n_norm:
            normed[t] = normalise(t, z.pop(t), ss.pop(t))
        t = step - 3
        if 0 <= t < n_norm:
            store(t, *normed.pop(t))


def _head_pair_constants():
    lane = np.arange(2 * HEAD_DIM)
    same_head = (lane[:, None] // HEAD_DIM) == (lane[None, :] // HEAD_DIM)
    partner = (lane // HEAD_DIM) * HEAD_DIM + (lane % HEAD_DIM + HEAD_DIM // 2) % HEAD_DIM
    perm = lane[:, None] == partner[None, :]
    return jnp.asarray(same_head, BF16), jnp.asarray(perm, BF16)


def _qkv_project(x, mod, mod_row, gmix, w_qkv, g_q, g_k, cos2, sin2, *, tm, use_rope):
    b, n, d = x.shape
    pair = 2 * HEAD_DIM
    ones, perm = _head_pair_constants()
    const = lambda bi, i: (0, 0)
    return pl.pallas_call(
        functools.partial(_qkv_kernel, use_rope=use_rope),
        grid=(b, n // tm),
        in_specs=[
            pl.BlockSpec((1, tm, d), lambda bi, i: (bi, i, 0)),
            pl.BlockSpec((1, N_MOD, d), lambda bi, i: (mod_row(bi), 0, 0)),
            pl.BlockSpec((1, d), const),
            pl.BlockSpec(w_qkv.shape, const),
            pl.BlockSpec((1, pair), const),
            pl.BlockSpec((1, pair), const),
            pl.BlockSpec((tm, pair), lambda bi, i: (i, 0)),
            pl.BlockSpec((tm, pair), lambda bi, i: (i, 0)),
            pl.BlockSpec((pair, pair), const),
            pl.BlockSpec((pair, pair), const),
        ],
        out_specs=[
            pl.BlockSpec((1, tm, d), lambda bi, i: (bi, i, 0)),
            pl.BlockSpec((1, tm, KV_WIDTH), lambda bi, i: (bi, i, 0)),
            pl.BlockSpec((1, N_KV_HEADS, HEAD_DIM, tm), lambda bi, i: (bi, 0, 0, i)),
        ],
        out_shape=[
            jax.ShapeDtypeStruct((b, n, d), BF16),
            jax.ShapeDtypeStruct((b, n, KV_WIDTH), BF16),
            jax.ShapeDtypeStruct((b, N_KV_HEADS, HEAD_DIM, n), BF16),
        ],
        compiler_params=_cparams(2),
        name="qkv_rope" if use_rope else "qkv_ctx",
    )(x, mod, gmix, w_qkv, g_q, g_k, cos2, sin2, ones, perm)


def _attend(q_ref, segments, o_ref, s_refs, base):
    tq, lanes = q_ref.shape[1], s_refs[0].shape[1]
    assert tq % lanes == 0
    units = [(slice(qs * lanes, (qs + 1) * lanes), slice(hh * HEAD_DIM, (hh + 1) * HEAD_DIM))
             for hh in range(Q_PER_KV) for qs in range(tq // lanes)]
    chunks, off = [], 0
    for k_seg, vt_seg in segments:
        nk = k_seg.shape[0]
        assert nk % 8 == 0
        for start in range(0, nk, KEY_CHUNK):
            size = min(KEY_CHUNK, nk - start)
            chunks.append((k_seg, vt_seg, start, off, size))
            off += size
    fold = lambda x, op: op(x.reshape(x.shape[0] // 8, 8, lanes), axis=0)

    def q_transposed(u):
        rows, hsl = units[u]
        return q_ref[0, rows, hsl].astype(F32).T.astype(BF16)

    def scores(u, qt, chunk, m8):
        k_seg, _, start, off, size = chunk
        s = jnp.dot(k_seg[start:start + size, :], qt, preferred_element_type=F32)
        s_refs[u % 2][off:off + size, :] = s
        ms = fold(s, jnp.max)
        return ms if m8 is None else jnp.maximum(m8, ms)

    def weigh(u, chunk, m, l8, acc):
        _, vt_seg, start, off, size = chunk
        s = s_refs[u % 2][pl.ds(pl.multiple_of(base + off, 8), size), :]
        p = jnp.exp2(s - m)
        ls = fold(p, jnp.sum)
        part = jnp.dot(vt_seg[:, start:start + size], p.astype(BF16), preferred_element_type=F32)
        return (ls if l8 is None else l8 + ls), (part if acc is None else acc + part)

    m8, qt = None, q_transposed(0)
    for chunk in chunks:
        m8 = scores(0, qt, chunk, m8)
    for u in range(len(units)):
        m = jnp.max(m8, axis=0, keepdims=True)
        m8, l8, acc = None, None, None
        if u + 1 < len(units):
            qt = q_transposed(u + 1)
        for chunk in chunks:
            if u + 1 < len(units):
                m8 = scores(u + 1, qt, chunk, m8)
            l8, acc = weigh(u, chunk, m, l8, acc)
        rows, hsl = units[u]
        l = jnp.sum(l8, axis=0, keepdims=True)
        o_ref[0, rows, hsl] = (acc * (1.0 / l)).T.astype(BF16)


def _attn_kernel(zero_ref, q_ref, k_ref, vt_ref, kc_ref, vtc_ref, o_ref, s0_ref, s1_ref):
    segments = [(k_ref.at[0], vt_ref.at[0, 0]), (kc_ref.at[0], vtc_ref.at[0, 0])]
    _attend(q_ref, segments, o_ref, (s0_ref, s1_ref), zero_ref[0])


def _ctx_attn_kernel(zero_ref, q_ref, kc_ref, vtc_ref, o_ref, s0_ref, s1_ref):
    _attend(q_ref, [(kc_ref.at[0], vtc_ref.at[0, 0])], o_ref, (s0_ref, s1_ref), zero_ref[0])


def _zero_offset():
    return jnp.zeros((1,), jnp.int32)


def _attn_scratch(n_keys, tq):
    lanes = min(QUERY_LANES, tq)
    return [pltpu.VMEM((n_keys, lanes), F32), pltpu.VMEM((n_keys, lanes), F32)]


def _attention(q, k, vt, kc, vtc, *, tq):
    b, s, d = q.shape
    n_ctx = kc.shape[1]
    gw = Q_PER_KV * HEAD_DIM
    return pl.pallas_call(
        _attn_kernel,
        grid=(b, N_KV_HEADS, s // tq),
        in_specs=[
            pl.BlockSpec(memory_space=pltpu.SMEM),
            pl.BlockSpec((1, tq, gw), lambda bi, g, i: (bi, i, g)),
            pl.BlockSpec((1, s, HEAD_DIM), lambda bi, g, i: (bi, 0, g)),
            pl.BlockSpec((1, 1, HEAD_DIM, s), lambda bi, g, i: (bi, g, 0, 0)),
            pl.BlockSpec((1, n_ctx, HEAD_DIM), lambda bi, g, i: (bi, 0, g)),
            pl.BlockSpec((1, 1, HEAD_DIM, n_ctx), lambda bi, g, i: (bi, g, 0, 0)),
        ],
        out_specs=pl.BlockSpec((1, tq, gw), lambda bi, g, i: (bi, i, g)),
        out_shape=jax.ShapeDtypeStruct((b, s, d), BF16),
        scratch_shapes=_attn_scratch(s + n_ctx, tq),
        compiler_params=_cparams(3),
        name="attention",
    )(_zero_offset(), q, k, vt, kc, vtc)


def _ctx_attention(qc, kc, vtc):
    b, n_ctx, d = qc.shape
    gw = Q_PER_KV * HEAD_DIM
    return pl.pallas_call(
        _ctx_attn_kernel,
        grid=(b, N_KV_HEADS),
        in_specs=[
            pl.BlockSpec(memory_space=pltpu.SMEM),
            pl.BlockSpec((1, n_ctx, gw), lambda bi, g: (bi, 0, g)),
            pl.BlockSpec((1, n_ctx, HEAD_DIM), lambda bi, g: (bi, 0, g)),
            pl.BlockSpec((1, 1, HEAD_DIM, n_ctx), lambda bi, g: (bi, g, 0, 0)),
        ],
        out_specs=pl.BlockSpec((1, n_ctx, gw), lambda bi, g: (bi, 0, g)),
        out_shape=jax.ShapeDtypeStruct((b, n_ctx, d), BF16),
        scratch_shapes=_attn_scratch(n_ctx, n_ctx),
        compiler_params=_cparams(2),
        name="ctx_attention",
    )(_zero_offset(), qc, kc, vtc)


def _post_kernel(x_ref, o_ref, mod_ref, wo_ref, bo_ref, gffn_ref, wgu_ref, wd_ref, gfin_ref,
                 out_ref, hn_ref, acc_ref, *, final_norm):
    proj = jnp.dot(o_ref[0], wo_ref[...], preferred_element_type=F32) + bo_ref[...]
    x1 = x_ref[0] + mod_ref[0, 2:3, :] * proj
    out_ref[0] = x1
    hn = _rms_mod(x1, gffn_ref[...], mod_ref[0, 3:4, :], mod_ref[0, 4:5, :])
    hn_ref[...] = hn.astype(BF16)
    acc_ref[...] = jnp.zeros(acc_ref.shape, F32)

    d_ff = wd_ref.shape[0]
    for c in range(d_ff // FF_CHUNK):
        gate = jnp.dot(hn_ref[...], wgu_ref[:, c * FF_CHUNK:(c + 1) * FF_CHUNK],
                       preferred_element_type=F32)
        up = jnp.dot(hn_ref[...], wgu_ref[:, d_ff + c * FF_CHUNK:d_ff + (c + 1) * FF_CHUNK],
                     preferred_element_type=F32)
        act = (_silu(gate) * up).astype(BF16)
        acc_ref[...] += jnp.dot(act, wd_ref[c * FF_CHUNK:(c + 1) * FF_CHUNK, :],
                                preferred_element_type=F32)
    y = out_ref[0] + mod_ref[0, 5:6, :] * acc_ref[...]
    if final_norm:
        y = y * lax.rsqrt(jnp.mean(y * y, axis=-1, keepdims=True) + EPS) * gfin_ref[...]
    out_ref[0] = y


def _post_mixer(x, o, mod, mod_row, w_o, b_o, g_ffn, wgu, wd, layer, g_fin, *, tm, final_norm):
    b, n, d = x.shape
    const2 = lambda bi, i: (0, 0)
    of_layer = lambda bi, i: (layer, 0, 0)
    single = pl.Buffered(1)
    return pl.pallas_call(
        functools.partial(_post_kernel, final_norm=final_norm),
        grid=(b, n // tm),
        in_specs=[
            pl.BlockSpec((1, tm, d), lambda bi, i: (bi, i, 0)),
            pl.BlockSpec((1, tm, d), lambda bi, i: (bi, i, 0)),
            pl.BlockSpec((1, N_MOD, d), lambda bi, i: (mod_row(bi), 0, 0)),
            pl.BlockSpec(w_o.shape, const2, pipeline_mode=single),
            pl.BlockSpec((1, d), const2),
            pl.BlockSpec((1, d), const2),
            pl.BlockSpec((None,) + wgu.shape[1:], of_layer, pipeline_mode=single),
            pl.BlockSpec((None,) + wd.shape[1:], of_layer, pipeline_mode=single),
            pl.BlockSpec((1, d), const2),
        ],
        out_specs=pl.BlockSpec((1, tm, d), lambda bi, i: (bi, i, 0)),
        out_shape=jax.ShapeDtypeStruct((b, n, d), F32),
        scratch_shapes=[pltpu.VMEM((tm, d), BF16), pltpu.VMEM((tm, d), F32)],
        compiler_params=_cparams(2),
        name="post_final" if final_norm else "post_mixer",
    )(x, o, mod, w_o, b_o, g_ffn, wgu, wd, g_fin)


def _dft1_kernel(x_ref, mod_ref, gmix_ref, g1_ref, twr_ref, twi_ref, t_ref):
    r, g, d = x_ref.shape[1:]
    rows = r * g
    x = x_ref[0].reshape(rows, d)
    h = _rms_mod(x, gmix_ref[...], mod_ref[0, 0:1, :], mod_ref[0, 1:2, :]).astype(BF16)
    a_re = jnp.dot(g1_ref[0], h, preferred_element_type=F32)
    a_im = jnp.dot(g1_ref[1], h, preferred_element_type=F32)
    c_re = twr_ref[...].reshape(rows, 128)
    c_im = twi_ref[...].reshape(rows, 128)
    for j in range(d // 128):
        sl = slice(j * 128, (j + 1) * 128)
        t_re = a_re[:, sl] * c_re - a_im[:, sl] * c_im
        t_im = a_re[:, sl] * c_im + a_im[:, sl] * c_re
        t_ref[0, 0, :, :, sl] = t_re.reshape(r, g, 128).astype(BF16)
        t_ref[0, 1, :, :, sl] = t_im.reshape(r, g, 128).astype(BF16)


def _dft2_kernel(t_ref, f2_ref, wc_ref, o_ref, y_ref):
    slabs, r, d = o_ref.shape[1:]
    n_groups = d // FOURIER_GROUP
    for j in range(slabs):
        u = jnp.concatenate([t_ref[0, 0, j], t_ref[0, 1, j]], axis=0)
        y = jnp.dot(f2_ref[...], u, preferred_element_type=F32).astype(BF16)
        for gi in range(n_groups):
            sl = slice(gi * FOURIER_GROUP, (gi + 1) * FOURIER_GROUP)
            y_ref[gi, j * r:(j + 1) * r, :FOURIER_GROUP] = y[:r, sl]
            y_ref[gi, j * r:(j + 1) * r, FOURIER_GROUP:] = y[r:, sl]
    for gi in range(n_groups):
        f = jnp.dot(y_ref[gi], wc_ref[...], preferred_element_type=F32)
        o_ref[0, :, :, gi * FOURIER_GROUP:(gi + 1) * FOURIER_GROUP] = (
            f.reshape(slabs, r, FOURIER_GROUP).astype(BF16))


def _dft_constants(s):
    r = DFT_RADIX
    assert r * r == s
    idx = np.arange(r)
    ang = 2.0 * np.pi * np.outer(idx, idx) / r
    eye = np.eye(DFT1_ROWS)
    norm = 1.0 / math.sqrt(s)
    g1 = np.stack([np.kron(np.cos(ang), eye), np.kron(-np.sin(ang), eye)]) * norm
    tw = 2.0 * np.pi * np.outer(idx, idx) / s
    twr = np.broadcast_to(np.cos(tw)[:, :, None], (r, r, 128))
    twi = np.broadcast_to(-np.sin(tw)[:, :, None], (r, r, 128))
    c2, s2 = np.cos(ang), np.sin(ang)
    f2 = np.block([[c2, s2], [-s2, c2]])
    gidx = np.arange(FOURIER_GROUP)
    gang = 2.0 * np.pi * np.outer(gidx, gidx) / FOURIER_GROUP
    wc = np.concatenate([np.cos(gang), np.sin(gang)], axis=0) / math.sqrt(FOURIER_GROUP)
    as_bf16 = lambda a: jnp.asarray(a, F32).astype(BF16)
    return (as_bf16(g1), jnp.asarray(twr, F32), jnp.asarray(twi, F32), as_bf16(f2), as_bf16(wc))


def _fourier_mix(x, mod, gmix):
    b, s, d = x.shape
    r = DFT_RADIX
    g1, twr, twi, f2, wc = _dft_constants(s)
    x4 = x.reshape(b, r, r, d)
    t = pl.pallas_call(
        _dft1_kernel,
        grid=(b, r // DFT1_ROWS),
        in_specs=[
            pl.BlockSpec((1, r, DFT1_ROWS, d), lambda bi, i: (bi, 0, i, 0)),
            pl.BlockSpec((1, N_MOD, d), lambda bi, i: (bi, 0, 0)),
            pl.BlockSpec((1, d), lambda bi, i: (0, 0)),
            pl.BlockSpec(g1.shape, lambda bi, i: (0, 0, 0)),
            pl.BlockSpec((r, DFT1_ROWS, 128), lambda bi, i: (0, i, 0)),
            pl.BlockSpec((r, DFT1_ROWS, 128), lambda bi, i: (0, i, 0)),
        ],
        out_specs=pl.BlockSpec((1, 2, r, DFT1_ROWS, d), lambda bi, i: (bi, 0, 0, i, 0)),
        out_shape=jax.ShapeDtypeStruct((b, 2, r, r, d), BF16),
        compiler_params=_cparams(2),
        name="dft_stage1",
    )(x4, mod, gmix, g1, twr, twi)
    fo = pl.pallas_call(
        _dft2_kernel,
        grid=(b, r // DFT2_SLABS),
        in_specs=[
            pl.BlockSpec((1, 2, DFT2_SLABS, r, d), lambda bi, i: (bi, 0, i, 0, 0)),
            pl.BlockSpec(f2.shape, lambda bi, i: (0, 0)),
            pl.BlockSpec(wc.shape, lambda bi, i: (0, 0)),
        ],
        out_specs=pl.BlockSpec((1, DFT2_SLABS, r, d), lambda bi, i: (bi, i, 0, 0)),
        out_shape=jax.ShapeDtypeStruct((b, r, r, d), BF16),
        scratch_shapes=[pltpu.VMEM((d // FOURIER_GROUP, DFT2_SLABS * r, 2 * FOURIER_GROUP), BF16)],
        compiler_params=_cparams(2),
        name="dft_stage2",
    )(t, f2, wc)
    return fo.transpose(0, 2, 1, 3).reshape(b, s, d)


def _rope_tables(s):
    rows = s // GRID_W
    row = jnp.repeat(jnp.arange(rows), GRID_W).astype(F32)
    col = jnp.tile(jnp.arange(GRID_W), rows).astype(F32)
    n_freq = HEAD_DIM // 4
    inv_freq = ROPE_THETA ** (-jnp.arange(n_freq, dtype=F32) / n_freq)
    ang = jnp.concatenate([row[:, None] * inv_freq, col[:, None] * inv_freq], axis=-1)
    cos, sin = jnp.cos(ang), jnp.sin(ang)
    return jnp.concatenate([cos, cos] * 2, axis=-1), jnp.concatenate([-sin, sin] * 2, axis=-1)


def kernel(x, c, ctx, c_ctx, w_mod, b_mod, g_mix, g_ffn, w_qkv, g_q, g_k, w_attn_out,
           w_fourier, b_fourier, w_gate_up, w_down, g_final):
    b, s, d = x.shape
    n_ctx = ctx.shape[1]
    depth = w_mod.shape[0]
    assert depth == 2 and b + 1 <= MOD_ROWS

    cc = jnp.concatenate([c, c_ctx[None, :], jnp.zeros((MOD_ROWS - b - 1, d), F32)], axis=0)
    mod = _modulation(cc, w_mod, b_mod).reshape(depth, MOD_ROWS, N_MOD, d)
    lat_row = lambda bi: bi
    ctx_row = lambda bi: b
    row = lambda v: v.reshape(1, -1)
    g_fin = row(g_final)
    zero_bias = jnp.zeros((1, d), F32)

    cos2, sin2 = _rope_tables(s)
    wqkv = w_qkv[0].astype(BF16)
    gq2, gk2 = jnp.tile(row(g_q[0]), (1, 2)), jnp.tile(row(g_k[0]), (1, 2))
    q, k, vt = _qkv_project(x, mod[0], lat_row, row(g_mix[0]), wqkv, gq2, gk2,
                            cos2, sin2, tm=QKV_ROWS, use_rope=True)
    qc, kc, vtc = _qkv_project(ctx, mod[0], ctx_row, row(g_mix[0]), wqkv, gq2, gk2,
                               cos2, sin2, tm=n_ctx, use_rope=False)
    o = _attention(q, k, vt, kc, vtc, tq=512)
    assert w_down.shape[1] % FF_CHUNK == 0 and w_gate_up.shape[2] == 2 * w_down.shape[1]
    wgu, wd = w_gate_up.astype(BF16), w_down.astype(BF16)
    wo = w_attn_out[0].astype(BF16)
    x = _post_mixer(x, o, mod[0], lat_row, wo, zero_bias, row(g_ffn[0]), wgu, wd, 0, g_fin,
                    tm=512, final_norm=False)
    oc = _ctx_attention(qc, kc, vtc)
    h_ctx = _post_mixer(ctx, oc, mod[0], ctx_row, wo, zero_bias, row(g_ffn[0]), wgu, wd, 0, g_fin,
                        tm=n_ctx, final_norm=False)
    del h_ctx

    f = _fourier_mix(x, mod[1], row(g_mix[1]))
    return _post_mixer(x, f, mod[1], lat_row, w_fourier[0].astype(BF16), row(b_fourier[0]),
                       row(g_ffn[1]), wgu, wd, 1, g_fin, tm=512, final_norm=True)
```
